```python
import jax, jax.numpy as jnp
from jax import lax
import numpy as np

D_MODEL = 1024
BATCH = 8
SEQ = 2048
DEPTH = 2
DEC_BATCH = 128
DEC_SEQ = 4
PAST_LEN = 16384
PAGE_SIZE = 128

D_A = 3 * D_MODEL // 8
A_HEADS = 4
A_HEAD_DIM = D_A // A_HEADS
CHUNK = 128
D_B = 3 * D_MODEL // 8
POOL_WINDOWS = (2, 4, 8, 16)
N_POOL_GROUPS = len(POOL_WINDOWS)
POOL_GD = D_B // N_POOL_GROUPS
MAX_WINDOW = max(POOL_WINDOWS)
POOL_BUF = MAX_WINDOW - 1
D_X = D_MODEL - D_A - D_B
X_HEADS = 4
X_HEAD_DIM = D_X // X_HEADS
N_MEM = 256
D_IN = 2 * D_A + D_B + D_X
D_FF = ((8 * D_MODEL // 3) + 255) // 256 * 256
N_EXPERTS = 8
TOP_K = 2
D_FF_E = 7 * D_MODEL // 2
N_DENSE = (DEPTH + 1) // 2
N_MOE = DEPTH // 2
EPS = 1e-6

kernel_name = "hybrid_gmlp_pool_memxattn_decode_step"


def rmsnorm(x, g):
    xf = x.astype(jnp.float32)
    y = xf * lax.rsqrt(jnp.mean(xf * xf, axis=-1, keepdims=True) + EPS)
    return (y * g.astype(jnp.float32)).astype(x.dtype)


def mixer_a(u, v, g_v, w_s, b_s, chunk_len):
    u = jax.nn.gelu(u)
    v = rmsnorm(jax.nn.gelu(v), g_v)
    bsz, t, _ = u.shape
    n_chunks = t // chunk_len
    causal = jnp.tril(jnp.ones((chunk_len, chunk_len), dtype=bool))
    w = jnp.where(causal[None], w_s[:, :chunk_len, :chunk_len], 0)
    vh = v.reshape(bsz, n_chunks, chunk_len, A_HEADS, A_HEAD_DIM)
    mixed = jnp.einsum('hts,bcshd->bcthd', w, vh) + b_s[:, :chunk_len].T[:, :, None]
    out = u * mixed.reshape(bsz, t, D_A)
    return out, v


def mixer_b(ext, n_prefix, pos0, w_pool, s_pool):
    bsz, length, _ = ext.shape
    t = length - n_prefix
    xf = ext.astype(jnp.float32)
    cs = jnp.concatenate([jnp.zeros((bsz, MAX_WINDOW + 1, D_B), jnp.float32),
                          jnp.cumsum(xf, axis=1)], axis=1)
    top = MAX_WINDOW + 1 + n_prefix
    upper = cs[:, top:top + t]
    pos = pos0 + jnp.arange(t, dtype=jnp.int32)
    parts = []
    for g, w in enumerate(POOL_WINDOWS):
        sl = slice(g * POOL_GD, (g + 1) * POOL_GD)
        lower = cs[:, top - w:top - w + t, sl]
        count = jnp.minimum(pos + 1, w).astype(jnp.float32)[None, :, None]
        parts.append((upper[..., sl] - lower) / count)
    mean = jnp.concatenate(parts, axis=-1)
    d = (mean - xf[:, n_prefix:]).reshape(bsz, t, N_POOL_GROUPS, POOL_GD)
    y = jnp.einsum('btgc,gcd->btgd', d, w_pool.astype(jnp.float32)).reshape(bsz, t, D_B)
    return (y * s_pool.astype(jnp.float32)).astype(ext.dtype)


def mem_kv(mem, g_mem, w_mem_kv):
    m = rmsnorm(mem, g_mem)
    kv = m @ w_mem_kv
    bsz = mem.shape[0]
    k = kv[..., :D_X].reshape(bsz, N_MEM, X_HEADS, X_HEAD_DIM)
    v = kv[..., D_X:].reshape(bsz, N_MEM, X_HEADS, X_HEAD_DIM)
    return k, v


def mixer_x(q, k, v):
    bsz, t, _ = q.shape
    qh = q.reshape(bsz, t, X_HEADS, X_HEAD_DIM).astype(jnp.float32)
    s = jnp.einsum('bthd,bmhd->bhtm', qh, k.astype(jnp.float32)) * (X_HEAD_DIM ** -0.5)
    p = jax.nn.softmax(s, axis=-1)
    o = jnp.einsum('bhtm,bmhd->bthd', p, v.astype(jnp.float32))
    return o.reshape(bsz, t, D_X).astype(q.dtype)


def mixing_sublayer(x, pool_prefix, pos0, chunk_len, mk, mv, g_norm, w_in, g_v, w_s, b_s,
                    w_pool, s_pool, g_group, w_out):
    h = rmsnorm(x, g_norm)
    p = h @ w_in
    u = p[..., :D_A]
    v = p[..., D_A:2 * D_A]
    xb = p[..., 2 * D_A:2 * D_A + D_B]
    q = p[..., 2 * D_A + D_B:]
    a_out, v_rows = mixer_a(u, v, g_v, w_s, b_s, chunk_len)
    if pool_prefix is None:
        ext, n_prefix = xb, 0
    else:
        ext, n_prefix = jnp.concatenate([pool_prefix.astype(xb.dtype), xb], axis=1), POOL_BUF
    b_out = mixer_b(ext, n_prefix, pos0, w_pool, s_pool)
    x_out = mixer_x(q, mk, mv)
    merged = jnp.concatenate([rmsnorm(a_out, g_group[:D_A]),
                              rmsnorm(b_out, g_group[D_A:D_A + D_B]),
                              rmsnorm(x_out, g_group[D_A + D_B:])], axis=-1)
    return x + merged @ w_out, v_rows, ext[:, -POOL_BUF:]


def swiglu(x, wg, wu, wd):
    return (jax.nn.silu(x @ wg) * (x @ wu)) @ wd


def moe_swiglu(x, w_router, wg, wu, wd):
    xf = x.reshape(-1, D_MODEL)
    logits = xf.astype(jnp.float32) @ w_router.astype(jnp.float32)
    vals, idx = lax.top_k(logits, TOP_K)
    wts = jax.nn.softmax(vals, axis=-1)
    gates = jnp.sum(jax.nn.one_hot(idx, N_EXPERTS, dtype=jnp.float32) * wts[..., None], axis=1)
    out = jnp.zeros(xf.shape, jnp.float32)
    for e in range(N_EXPERTS):
        out = out + gates[:, e:e + 1] * swiglu(xf, wg[e], wu[e], wd[e]).astype(jnp.float32)
    return out.astype(x.dtype).reshape(x.shape)


def channel_sublayer(x, l, g_ffn, w_dense_gate, w_dense_up, w_dense_down,
                     w_router, w_moe_gate, w_moe_up, w_moe_down):
    h = rmsnorm(x, g_ffn[l])
    i = l // 2
    if l % 2 == 0:
        f = swiglu(h, w_dense_gate[i], w_dense_up[i], w_dense_down[i])
    else:
        f = moe_swiglu(h, w_router[i], w_moe_gate[i], w_moe_up[i], w_moe_down[i])
    return x + f


def setup_inputs(seed: int = 0) -> dict:
    key = jax.random.key(seed)
    ks = iter(jax.random.split(key, 32))
    f32 = jnp.float32

    def nrm(shape, scale):
        return jax.random.normal(next(ks), shape, f32) * scale

    def gain(shape):
        return 1.0 + 0.05 * jax.random.normal(next(ks), shape, f32)

    return {
        "x_prompt": nrm((BATCH, SEQ, D_MODEL), 1.0),
        "x_sample": nrm((DEC_BATCH, DEC_SEQ, D_MODEL), 1.0),
        "mem_prompt": nrm((BATCH, N_MEM, D_MODEL), 1.0),
        "state_pool": nrm((DEPTH, DEC_BATCH, POOL_BUF, D_B), 1.0),
        "cache_mem_k": nrm((DEPTH, DEC_BATCH, N_MEM, X_HEADS, X_HEAD_DIM), 1.0),
        "cache_mem_v": nrm((DEPTH, DEC_BATCH, N_MEM, X_HEADS, X_HEAD_DIM), 1.0),
        "g_mix_norm": gain((DEPTH, D_MODEL)),
        "w_in": nrm((DEPTH, D_MODEL, D_IN), D_MODEL ** -0.5),
        "g_v": gain((DEPTH, D_A)),
        "w_spatial": nrm((DEPTH, A_HEADS, CHUNK, CHUNK), 0.5 * CHUNK ** -0.5),
        "b_spatial": gain((DEPTH, A_HEADS, CHUNK)),
        "w_pool": nrm((DEPTH, N_POOL_GROUPS, POOL_GD, POOL_GD), POOL_GD ** -0.5),
        "s_pool": gain((DEPTH, D_B)),
        "g_mem": gain((DEPTH, D_MODEL)),
        "w_mem_kv": nrm((DEPTH, D_MODEL, 2 * D_X), D_MODEL ** -0.5),
        "g_group": gain((DEPTH, D_MODEL)),
        "w_out": nrm((DEPTH, D_MODEL, D_MODEL), D_MODEL ** -0.5),
        "g_ffn_norm": gain((DEPTH, D_MODEL)),
        "w_dense_gate": nrm((N_DENSE, D_MODEL, D_FF), D_MODEL ** -0.5),
        "w_dense_up": nrm((N_DENSE, D_MODEL, D_FF), D_MODEL ** -0.5),
        "w_dense_down": nrm((N_DENSE, D_FF, D_MODEL), D_FF ** -0.5),
        "w_router": nrm((N_MOE, D_MODEL, N_EXPERTS), D_MODEL ** -0.5),
        "w_moe_gate": nrm((N_MOE, N_EXPERTS, D_MODEL, D_FF_E), D_MODEL ** -0.5),
        "w_moe_up": nrm((N_MOE, N_EXPERTS, D_MODEL, D_FF_E), D_MODEL ** -0.5),
        "w_moe_down": nrm((N_MOE, N_EXPERTS, D_FF_E, D_MODEL), D_FF_E ** -0.5),
        "g_final": gain((D_MODEL,)),
    }


def reference(x_prompt, x_sample, mem_prompt, state_pool, cache_mem_k, cache_mem_v,
              g_mix_norm, w_in, g_v, w_spatial, b_spatial, w_pool, s_pool, g_mem, w_mem_kv,
              g_group, w_out, g_ffn_norm, w_dense_gate, w_dense_up, w_dense_down,
              w_router, w_moe_gate, w_moe_up, w_moe_down, g_final):
    xp = x_prompt
    xs = x_sample
    dec_seq = x_sample.shape[1]
    pool_p, mk_p, mv_p, pool_s, vrows_s = [], [], [], [], []
    for l in range(DEPTH):
        mix_w = (g_mix_norm[l], w_in[l], g_v[l], w_spatial[l], b_spatial[l],
                 w_pool[l], s_pool[l], g_group[l], w_out[l])
        mk, mv = mem_kv(mem_prompt, g_mem[l], w_mem_kv[l])
        xp, _, buf_p = mixing_sublayer(xp, None, 0, CHUNK, mk, mv, *mix_w)
        mk_p.append(mk)
        mv_p.append(mv)
        pool_p.append(buf_p)
        xs, v_rows, buf_s = mixing_sublayer(xs, state_pool[l], PAST_LEN, dec_seq,
                                            cache_mem_k[l], cache_mem_v[l], *mix_w)
        pool_s.append(buf_s)
        vrows_s.append(v_rows)
        ffn_w = (g_ffn_norm, w_dense_gate, w_dense_up, w_dense_down,
                 w_router, w_moe_gate, w_moe_up, w_moe_down)
        xp = channel_sublayer(xp, l, *ffn_w)
        xs = channel_sublayer(xs, l, *ffn_w)
    y_prompt = rmsnorm(xp, g_final)
    y_sample = rmsnorm(xs, g_final)
    return (y_prompt, y_sample, jnp.stack(pool_p), jnp.stack(mk_p), jnp.stack(mv_p),
            jnp.stack(pool_s), jnp.stack(vrows_s))
```

```python
import functools

import jax
import jax.numpy as jnp
from jax import lax
from jax.experimental import pallas as pl
from jax.experimental.pallas import tpu as pltpu
from jax.scipy.linalg import block_diag

F32 = jnp.float32
BF16 = jnp.bfloat16
I32 = jnp.int32

EPS = 1e-6
CHUNK = 128
A_HEADS = 4
POOL_WINDOWS = (2, 4, 8, 16)
POOL_BUF = max(POOL_WINDOWS) - 1
POOL_HDR = 16
X_HEADS = 4
N_EXPERTS = 8
PAST_LEN = 16384

TQ = 512
TM_FFN = 1024
TF_DENSE = 1408
TF_MOE = 512
TR = 512
ATT_G = 8
VMEM_LIMIT = 56 * 1024 * 1024


def _cparams(sem):
    return pltpu.CompilerParams(dimension_semantics=sem, vmem_limit_bytes=VMEM_LIMIT)


def _rms(x, g):
    ms = jnp.mean(x * x, axis=-1, keepdims=True)
    return (x * lax.rsqrt(ms + EPS)) * g


def _dot(a, b):
    return jnp.dot(a, b, preferred_element_type=F32)


def _dot_nt(a, b):
    return lax.dot_general(a, b, (((1,), (1,)), ((), ())), preferred_element_type=F32)


def _lane_group_mask(width, n_groups, g):
    lane = lax.broadcasted_iota(I32, (1, width), 1)
    gw = width // n_groups
    return (lane >= g * gw) & (lane < (g + 1) * gw)


def _select_by_group(parts, width):
    out = parts[-1]
    for g in range(len(parts) - 2, -1, -1):
        out = jnp.where(_lane_group_mask(width, len(parts), g), parts[g], out)
    return out


def _softmax_rows(s):
    m = jnp.max(s, axis=-1, keepdims=True)
    e = jnp.exp(s - m)
    return e / jnp.sum(e, axis=-1, keepdims=True)


def _silu(x):
    return x * (1.0 / (1.0 + jnp.exp(-x)))


def _memkv_kernel(mem_ref, g_ref, w_ref, k_ref, v_ref):
    dx = k_ref.shape[-1]
    m = _rms(mem_ref[...], g_ref[...]).astype(BF16)
    kv = _dot(m, w_ref[...])
    k_ref[...] = kv[:, :dx]
    v_ref[...] = kv[:, dx:]


def _mem_kv(mem, g_mem, w_mem_kv_bf16):
    depth, d, two_dx = w_mem_kv_bf16.shape
    b, n_mem, _ = mem.shape
    dx = two_dx // 2
    out = jax.ShapeDtypeStruct((depth, b, n_mem, dx), F32)
    return pl.pallas_call(
        _memkv_kernel,
        out_shape=(out, out),
        grid=(depth, b),
        in_specs=[
            pl.BlockSpec((None, n_mem, d), lambda l, i: (i, 0, 0)),
            pl.BlockSpec((None, 1, d), lambda l, i: (l, 0, 0)),
            pl.BlockSpec((None, d, two_dx), lambda l, i: (l, 0, 0)),
        ],
        out_specs=(
            pl.BlockSpec((None, None, n_mem, dx), lambda l, i: (l, i, 0, 0)),
            pl.BlockSpec((None, None, n_mem, dx), lambda l, i: (l, i, 0, 0)),
        ),
        compiler_params=_cparams(("arbitrary", "arbitrary")),
        name="mem_kv",
    )(mem, g_mem.reshape(depth, 1, d), w_mem_kv_bf16)


def _mix_prompt_kernel(x_ref, gn_ref, win_ref, gv_ref, ws_ref, bias_ref, wpool_ref, spool_ref,
                       gg_ref, mk_ref, mv_ref, wout_ref, xo_ref, pool_ref, ext_ref, *, d_a, d_b, d_x):
    t = pl.program_id(1)
    n_t = pl.num_programs(1)
    tq = x_ref.shape[0]

    x = x_ref[...]
    h = _rms(x, gn_ref[...]).astype(BF16)
    p = _dot(h, win_ref[...])
    u = p[:, :d_a]
    v = p[:, d_a:2 * d_a]
    xb = p[:, 2 * d_a:2 * d_a + d_b]
    q = p[:, 2 * d_a + d_b:]

    ug = jax.nn.gelu(u)
    vn = _rms(jax.nn.gelu(v), gv_ref[...])
    row = lax.broadcasted_iota(I32, (CHUNK, A_HEADS * CHUNK), 0)
    col = lax.broadcasted_iota(I32, (CHUNK, A_HEADS * CHUNK), 1)
    wcat = jnp.concatenate([ws_ref[hh] for hh in range(A_HEADS)], axis=1)
    wcat = jnp.where((col & (CHUNK - 1)) <= row, wcat, 0.0).astype(BF16)
    head_masks = [_lane_group_mask(d_a, A_HEADS, hh) for hh in range(A_HEADS)]
    a_parts = []
    for c in range(tq // CHUNK):
        vc = vn[c * CHUNK:(c + 1) * CHUNK]
        vbd = jnp.concatenate([jnp.where(hm, vc, 0.0) for hm in head_masks], axis=0).astype(BF16)
        mixed = _dot(wcat, vbd) + bias_ref[...]
        a_parts.append(ug[c * CHUNK:(c + 1) * CHUNK] * mixed)
    a_out = jnp.concatenate(a_parts, axis=0)

    @pl.when(t == 0)
    def _():
        ext_ref[0:POOL_HDR, :] = jnp.zeros((POOL_HDR, d_b), F32)

    ext_ref[POOL_HDR:POOL_HDR + tq, :] = xb
    pos = t * tq + lax.broadcasted_iota(I32, (tq, 1), 0)
    acc = xb
    means = []
    k = 1
    for w in POOL_WINDOWS:
        while k < w:
            acc = acc + ext_ref[POOL_HDR - k:POOL_HDR - k + tq, :]
            k += 1
        cnt = jnp.minimum(pos + 1, w).astype(F32)
        means.append(acc / cnt)
    dlt = _select_by_group(means, d_b) - xb
    b_out = _dot(dlt.astype(BF16), wpool_ref[...]) * spool_ref[...]
    tail = ext_ref[tq:tq + POOL_HDR, :]
    ext_ref[0:POOL_HDR, :] = tail

    @pl.when(t == n_t - 1)
    def _():
        pool_ref[...] = tail

    kf = mk_ref[...]
    vf = mv_ref[...]
    qb = q.astype(BF16)
    scale = (d_x // X_HEADS) ** -0.5
    o = jnp.zeros((tq, d_x), F32)
    for hh in range(X_HEADS):
        hm = _lane_group_mask(d_x, X_HEADS, hh)
        kh = jnp.where(hm, kf, 0.0).astype(BF16)
        s = _dot_nt(qb, kh) * scale
        pr = _softmax_rows(s).astype(BF16)
        vh = jnp.where(hm, vf, 0.0).astype(BF16)
        o = o + _dot(pr, vh)

    gg = gg_ref[...]
    merged = jnp.concatenate([
        _rms(a_out, gg[:, :d_a]),
        _rms(b_out, gg[:, d_a:d_a + d_b]),
        _rms(o, gg[:, d_a + d_b:]),
    ], axis=-1).astype(BF16)
    xo_ref[...] = x + _dot(merged, wout_ref[...])


def _mix_prompt(x, mk, mv, g_norm, w_in_bf16, g_v, w_s, bias_full, wpool_bd, s_pool, g_group,
                w_out_bf16):
    b, t, d = x.shape
    d_in = w_in_bf16.shape[1]
    d_a = g_v.shape[0]
    d_b = s_pool.shape[0]
    d_x = d - d_a - d_b
    n_mem = mk.shape[1]
    full = lambda shape: pl.BlockSpec(shape, lambda i, j: (0,) * len(shape))
    kern = functools.partial(_mix_prompt_kernel, d_a=d_a, d_b=d_b, d_x=d_x)
    xo, pool = pl.pallas_call(
        kern,
        out_shape=(jax.ShapeDtypeStruct((b, t, d), F32),
                   jax.ShapeDtypeStruct((b, POOL_HDR, d_b), F32)),
        grid=(b, t // TQ),
        in_specs=[
            pl.BlockSpec((None, TQ, d), lambda i, j: (i, j, 0)),
            full((1, d)),
            full((d, d_in)),
            full((1, d_a)),
            full((A_HEADS, CHUNK, CHUNK)),
            full((CHUNK, d_a)),
            full((d_b, d_b)),
            full((1, d_b)),
            full((1, d)),
            pl.BlockSpec((None, n_mem, d_x), lambda i, j: (i, 0, 0)),
            pl.BlockSpec((None, n_mem, d_x), lambda i, j: (i, 0, 0)),
            full((d, d)),
        ],
        out_specs=(
            pl.BlockSpec((None, TQ, d), lambda i, j: (i, j, 0)),
            pl.BlockSpec((None, POOL_HDR, d_b), lambda i, j: (i, 0, 0)),
        ),
        scratch_shapes=[pltpu.VMEM((POOL_HDR + TQ, d_b), F32)],
        compiler_params=_cparams(("arbitrary", "arbitrary")),
        name="mix_prompt",
    )(x, g_norm.reshape(1, d), w_in_bf16, g_v.reshape(1, d_a), w_s, bias_full, wpool_bd,
      s_pool.reshape(1, d_b), g_group.reshape(1, d), mk, mv, w_out_bf16)
    return xo, pool[:, POOL_HDR - POOL_BUF:]


def _log2(n):
    assert n & (n - 1) == 0, n
    return n.bit_length() - 1


def _head_replication(n_req, n_new, transposed):
    hr = n_req * X_HEADS * n_new
    tr = n_new * n_req
    shape = (tr, hr) if transposed else (hr, tr)
    r = lax.broadcasted_iota(I32, shape, 1 if transposed else 0)
    c = lax.broadcasted_iota(I32, shape, 0 if transposed else 1)
    return c == (r & (n_new - 1)) * n_req + (r >> _log2(X_HEADS * n_new))


def _row_head_mask(rows, n_new, d_x):
    r = lax.broadcasted_iota(I32, (rows, d_x), 0)
    lane = lax.broadcasted_iota(I32, (rows, d_x), 1)
    return ((r >> _log2(n_new)) & (X_HEADS - 1)) == (lane >> _log2(d_x // X_HEADS))


def _mix_sample_pre_kernel(x_ref, gn_ref, win_ref, gv_ref, wv_ref, bv_ref, state_ref, wpool_ref,
                           spool_ref, vn_ref, a_ref, b_ref, qm_ref, pool_ref, *, n_new, d_a, d_b):
    n_req = x_ref.shape[0] // n_new
    h = _rms(x_ref[...], gn_ref[...]).astype(BF16)
    p = _dot(h, win_ref[...])
    ug = jax.nn.gelu(p[:, :d_a])
    vn = _rms(jax.nn.gelu(p[:, d_a:2 * d_a]), gv_ref[...])
    xb = p[:, 2 * d_a:2 * d_a + d_b]
    q = p[:, 2 * d_a + d_b:]
    d_x = q.shape[1]
    vn_ref[...] = vn
    slab = lambda arr, i: arr[i * n_req:(i + 1) * n_req]

    a_parts = []
    for t in range(n_new):
        mixed = bv_ref[t:t + 1, :]
        for s in range(t + 1):
            mixed = mixed + wv_ref[t * n_new + s:t * n_new + s + 1, :] * slab(vn, s)
        a_parts.append(slab(ug, t) * mixed)
    a_ref[...] = jnp.concatenate(a_parts, axis=0)

    suffix = [jnp.zeros((n_req, d_b), F32)]
    for j in range(1, POOL_BUF + 1):
        suffix.append(suffix[-1] + state_ref[POOL_BUF - j])
    d_parts = []
    for t in range(n_new):
        means = []
        for w in POOL_WINDOWS:
            acc = suffix[max(0, w - 1 - t)]
            for s in range(max(0, t - w + 1), t + 1):
                acc = acc + slab(xb, s)
            means.append(acc / float(min(PAST_LEN + t + 1, w)))
        d_parts.append(_select_by_group(means, d_b) - slab(xb, t))
    dlt = jnp.concatenate(d_parts, axis=0).astype(BF16)
    b_ref[...] = _dot(dlt, wpool_ref[...]) * spool_ref[...]

    for i in range(POOL_BUF - n_new):
        pool_ref[i] = state_ref[i + n_new]
    for t in range(n_new):
        pool_ref[POOL_BUF - n_new + t] = slab(xb, t)

    rep = jnp.where(_head_replication(n_req, n_new, False), 1.0, 0.0).astype(BF16)
    qm = _dot(rep, q.astype(BF16))
    qm_ref[...] = jnp.where(_row_head_mask(qm.shape[0], n_new, d_x), qm, 0.0).astype(BF16)


def _mix_sample_pre(x, state, g_norm, w_in_bf16, g_v, wv, bv, wpool_bd, s_pool, n_new):
    n, d = x.shape
    n_req = n // n_new
    d_a = g_v.shape[0]
    d_b = s_pool.shape[0]
    d_x = d - d_a - d_b
    kern = functools.partial(_mix_sample_pre_kernel, n_new=n_new, d_a=d_a, d_b=d_b)
    return pl.pallas_call(
        kern,
        out_shape=(
            jax.ShapeDtypeStruct((n, d_a), F32),
            jax.ShapeDtypeStruct((n, d_a), F32),
            jax.ShapeDtypeStruct((n, d_b), F32),
            jax.ShapeDtypeStruct((n_req * X_HEADS * n_new, d_x), BF16),
            jax.ShapeDtypeStruct((POOL_BUF, n_req, d_b), F32),
        ),
        compiler_params=pltpu.CompilerParams(vmem_limit_bytes=VMEM_LIMIT),
        name="mix_sample_pre",
    )(x, g_norm.reshape(1, d), w_in_bf16, g_v.reshape(1, d_a), wv, bv, state, wpool_bd,
      s_pool.reshape(1, d_b))


def _attn_sample_kernel(qm_ref, k_ref, v_ref, o_ref, *, rows_per_req, scale):
    for g in range(k_ref.shape[0]):
        qg = qm_ref[g * rows_per_req:(g + 1) * rows_per_req, :]
        s = _dot_nt(qg, k_ref[g].astype(BF16)) * scale
        pr = _softmax_rows(s).astype(BF16)
        o_ref[g * rows_per_req:(g + 1) * rows_per_req, :] = _dot(pr, v_ref[g].astype(BF16))


def _attn_sample(qm, k, v, n_new):
    n_req, n_mem, d_x = k.shape
    rpr = X_HEADS * n_new
    kern = functools.partial(_attn_sample_kernel, rows_per_req=rpr,
                             scale=(d_x // X_HEADS) ** -0.5)
    return pl.pallas_call(
        kern,
        out_shape=jax.ShapeDtypeStruct((n_req * rpr, d_x), F32),
        grid=(n_req // ATT_G,),
        in_specs=[
            pl.BlockSpec((ATT_G * rpr, d_x), lambda i: (i, 0)),
            pl.BlockSpec((ATT_G, n_mem, d_x), lambda i: (i, 0, 0)),
            pl.BlockSpec((ATT_G, n_mem, d_x), lambda i: (i, 0, 0)),
        ],
        out_specs=pl.BlockSpec((ATT_G * rpr, d_x), lambda i: (i, 0)),
        compiler_params=_cparams(("arbitrary",)),
        name="attn_sample",
    )(qm, k, v)


def _split_bf16(x):
    hi = x.astype(BF16)
    r1 = x - hi.astype(F32)
    mid = r1.astype(BF16)
    lo = (r1 - mid.astype(F32)).astype(BF16)
    return hi, mid, lo


def _mix_sample_post_kernel(x_ref, a_ref, b_ref, o_ref, gg_ref, wout_ref, xo_ref, *, n_new):
    n = x_ref.shape[0]
    d_a = a_ref.shape[1]
    d_b = b_ref.shape[1]
    d_x = o_ref.shape[1]
    om = jnp.where(_row_head_mask(o_ref.shape[0], n_new, d_x), o_ref[...], 0.0)
    rep_t = jnp.where(_head_replication(n // n_new, n_new, True), 1.0, 0.0).astype(BF16)
    hi, mid, lo = _split_bf16(om)
    x_att = _dot(rep_t, hi) + _dot(rep_t, mid) + _dot(rep_t, lo)
    gg = gg_ref[...]
    merged = jnp.concatenate([
        _rms(a_ref[...], gg[:, :d_a]),
        _rms(b_ref[...], gg[:, d_a:d_a + d_b]),
        _rms(x_att, gg[:, d_a + d_b:]),
    ], axis=-1).astype(BF16)
    xo_ref[...] = x_ref[...] + _dot(merged, wout_ref[...])


def _mix_sample_post(x, a_out, b_out, o_all, g_group, w_out_bf16, n_new):
    n, d = x.shape
    kern = functools.partial(_mix_sample_post_kernel, n_new=n_new)
    return pl.pallas_call(
        kern,
        out_shape=jax.ShapeDtypeStruct((n, d), F32),
        compiler_params=pltpu.CompilerParams(vmem_limit_bytes=VMEM_LIMIT),
        name="mix_sample_post",
    )(x, a_out, b_out, o_all, g_group.reshape(1, d), w_out_bf16)


def _ffn_dense_kernel(x_ref, g_ref, wg_ref, wu_ref, wd_ref, o_ref, hn_ref):
    j = pl.program_id(1)

    @pl.when(j == 0)
    def _():
        hn_ref[...] = _rms(x_ref[...], g_ref[...]).astype(BF16)

    hn = hn_ref[...]
    gate = _dot(hn, wg_ref[...].astype(BF16))
    up = _dot(hn, wu_ref[...].astype(BF16))
    y = _dot((_silu(gate) * up).astype(BF16), wd_ref[...].astype(BF16))

    @pl.when(j == 0)
    def _():
        o_ref[...] = x_ref[...] + y

    @pl.when(j > 0)
    def _():
        o_ref[...] += y


def _ffn_dense(x, g, wg, wu, wd, tm):
    n, d = x.shape
    f = wg.shape[1]
    return pl.pallas_call(
        _ffn_dense_kernel,
        out_shape=jax.ShapeDtypeStruct((n, d), F32),
        grid=(n // tm, f // TF_DENSE),
        in_specs=[
            pl.BlockSpec((tm, d), lambda i, j: (i, 0)),
            pl.BlockSpec((1, d), lambda i, j: (0, 0)),
            pl.BlockSpec((d, TF_DENSE), lambda i, j: (0, j)),
            pl.BlockSpec((d, TF_DENSE), lambda i, j: (0, j)),
            pl.BlockSpec((TF_DENSE, d), lambda i, j: (j, 0)),
        ],
        out_specs=pl.BlockSpec((tm, d), lambda i, j: (i, 0)),
        scratch_shapes=[pltpu.VMEM((tm, d), BF16)],
        compiler_params=_cparams(("arbitrary", "arbitrary")),
        name="ffn_dense",
    )(x, g.reshape(1, d), wg, wu, wd)


def _router_kernel(xp_ref, xs_ref, g_ref, wr_ref, hn_ref, eid_ref, gate_ref, rank_ref, cnt_ref,
                   carry_ref, *, n_prompt_tiles):
    i = pl.program_id(0)

    @pl.when(i == 0)
    def _():
        carry_ref[...] = jnp.zeros_like(carry_ref)

    x = jnp.where(i < n_prompt_tiles, xp_ref[...], xs_ref[...])
    hn = _rms(x, g_ref[...])
    hn_ref[...] = hn
    logits = jnp.dot(hn, wr_ref[...], preferred_element_type=F32,
                     precision=lax.Precision.HIGHEST)
    n_e = logits.shape[1]
    lane = lax.broadcasted_iota(I32, logits.shape, 1).astype(F32)
    m1 = jnp.max(logits, axis=-1, keepdims=True)
    i1 = jnp.min(jnp.where(logits == m1, lane, float(n_e)), axis=-1, keepdims=True)
    rest = jnp.where(lane == i1, -jnp.inf, logits)
    m2 = jnp.max(rest, axis=-1, keepdims=True)
    i2 = jnp.min(jnp.where(rest == m2, lane, float(n_e)), axis=-1, keepdims=True)
    e2 = jnp.exp(m2 - m1)
    den = 1.0 + e2
    two = lax.broadcasted_iota(I32, (logits.shape[0], 2), 1)
    eid_ref[...] = jnp.where(two == 0, i1, i2).astype(I32)
    gate_ref[...] = jnp.where(two == 0, 1.0 / den, e2 / den)

    onehot = jnp.where((lane == i1) | (lane == i2), 1.0, 0.0)
    tr = logits.shape[0]
    r = lax.broadcasted_iota(I32, (tr, tr), 0)
    c = lax.broadcasted_iota(I32, (tr, tr), 1)
    earlier = jnp.where(c < r, 1.0, 0.0).astype(BF16)
    before = _dot(earlier, onehot.astype(BF16)) + carry_ref[...]
    r1 = jnp.sum(jnp.where(lane == i1, before, 0.0), axis=-1, keepdims=True)
    r2 = jnp.sum(jnp.where(lane == i2, before, 0.0), axis=-1, keepdims=True)
    rank_ref[...] = jnp.where(two == 0, r1, r2).astype(I32)
    carry_ref[...] += jnp.sum(onehot, axis=0, keepdims=True)
    cnt_ref[...] = carry_ref[...].astype(I32)


def _router(xp, xs, g, w_router):
    n_p, d = xp.shape
    n_s = xs.shape[0]
    assert n_s == TR and n_p % TR == 0
    n = n_p + n_s
    npt = n_p // TR
    n_e = w_router.shape[1]
    kern = functools.partial(_router_kernel, n_prompt_tiles=npt)
    tok2 = lambda dt: jax.ShapeDtypeStruct((n, 2), dt)
    return pl.pallas_call(
        kern,
        out_shape=(jax.ShapeDtypeStruct((n, d), F32), tok2(I32), tok2(F32), tok2(I32),
                   jax.ShapeDtypeStruct((1, n_e), I32)),
        grid=(n // TR,),
        in_specs=[
            pl.BlockSpec((TR, d), lambda i: (jnp.minimum(i, npt - 1), 0)),
            pl.BlockSpec((TR, d), lambda i: (0, 0)),
            pl.BlockSpec((1, d), lambda i: (0, 0)),
            pl.BlockSpec((d, n_e), lambda i: (0, 0)),
        ],
        out_specs=(
            pl.BlockSpec((TR, d), lambda i: (i, 0)),
            pl.BlockSpec((TR, 2), lambda i: (i, 0)),
            pl.BlockSpec((TR, 2), lambda i: (i, 0)),
            pl.BlockSpec((TR, 2), lambda i: (i, 0)),
            pl.BlockSpec((1, n_e), lambda i: (0, 0)),
        ),
        scratch_shapes=[pltpu.VMEM((1, n_e), F32)],
        compiler_params=_cparams(("arbitrary",)),
        name="moe_router",
    )(xp, xs, g.reshape(1, d), w_router)


def _scatter_kernel(slot_ref, hn_ref, xs_in_ref, xs_ref, sem):
    del xs_in_ref
    i = pl.program_id(0)
    base = i * TR

    def row_copy(tok, slot):
        return pltpu.make_async_copy(hn_ref.at[pl.ds(tok, 1)], xs_ref.at[pl.ds(slot, 1)], sem)

    def issue(r, carry):
        tok = base + r
        row_copy(tok, slot_ref[2 * tok]).start()
        row_copy(tok, slot_ref[2 * tok + 1]).start()
        return carry

    lax.fori_loop(0, TR, issue, 0)

    def drain(r, carry):
        row_copy(0, 0).wait()
        row_copy(0, 0).wait()
        return carry

    lax.fori_loop(0, TR, drain, 0)


def _scatter_rows(slot_flat, hn, n_slots):
    n, d = hn.shape
    xs0 = jnp.zeros((n_slots, d), F32)
    return pl.pallas_call(
        _scatter_kernel,
        out_shape=jax.ShapeDtypeStruct((n_slots, d), F32),
        grid_spec=pltpu.PrefetchScalarGridSpec(
            num_scalar_prefetch=1,
            grid=(n // TR,),
            in_specs=[pl.BlockSpec(memory_space=pl.ANY), pl.BlockSpec(memory_space=pl.ANY)],
            out_specs=pl.BlockSpec(memory_space=pl.ANY),
            scratch_shapes=[pltpu.SemaphoreType.DMA],
        ),
        input_output_aliases={2: 0},
        compiler_params=_cparams(("arbitrary",)),
        name="moe_scatter",
    )(slot_flat, hn, xs0)


def _ffn_moe_kernel(te_ref, nv_ref, x_ref, wg_ref, wu_ref, wd_ref, o_ref, xb_ref):
    i = pl.program_id(0)
    j = pl.program_id(1)

    @pl.when(i < nv_ref[0])
    def _():
        @pl.when(j == 0)
        def _():
            xb_ref[...] = x_ref[...].astype(BF16)

        xb = xb_ref[...]
        gate = _dot(xb, wg_ref[...].astype(BF16))
        up = _dot(xb, wu_ref[...].astype(BF16))
        y = _dot((_silu(gate) * up).astype(BF16), wd_ref[...].astype(BF16))

        @pl.when(j == 0)
        def _():
            o_ref[...] = y

        @pl.when(j > 0)
        def _():
            o_ref[...] += y

    @pl.when((i >= nv_ref[0]) & (j == 0))
    def _():
        o_ref[...] = jnp.zeros_like(o_ref)


def _ffn_moe(tile_expert, n_valid, xs, wg, wu, wd):
    n_slots, d = xs.shape
    n_e, _, f = wg.shape
    nt = n_slots // TM_FFN
    nj = f // TF_MOE

    def row_idx(i, j, te, nv):
        return (jnp.minimum(i, nv[0] - 1), 0)

    def out_idx(i, j, te, nv):
        return (i, 0)

    def w_idx(i, j, te, nv):
        ok = i < nv[0]
        return (te[jnp.minimum(i, nv[0] - 1)], 0, jnp.where(ok, j, nj - 1))

    def wd_idx(i, j, te, nv):
        ok = i < nv[0]
        return (te[jnp.minimum(i, nv[0] - 1)], jnp.where(ok, j, nj - 1), 0)

    return pl.pallas_call(
        _ffn_moe_kernel,
        out_shape=jax.ShapeDtypeStruct((n_slots, d), F32),
        grid_spec=pltpu.PrefetchScalarGridSpec(
            num_scalar_prefetch=2,
            grid=(nt, nj),
            in_specs=[
                pl.BlockSpec((TM_FFN, d), row_idx),
                pl.BlockSpec((None, d, TF_MOE), w_idx),
                pl.BlockSpec((None, d, TF_MOE), w_idx),
                pl.BlockSpec((None, TF_MOE, d), wd_idx),
            ],
            out_specs=pl.BlockSpec((TM_FFN, d), out_idx),
            scratch_shapes=[pltpu.VMEM((TM_FFN, d), BF16)],
        ),
        compiler_params=_cparams(("arbitrary", "arbitrary")),
        name="ffn_moe",
    )(tile_expert, n_valid, xs, wg, wu, wd)


def _combine_kernel(slot_ref, xp_ref, xs_ref, gate_ref, gf_ref, ys_ref, yp_ref, ysm_ref, buf_ref,
                    sem, *, n_prompt_tiles):
    i = pl.program_id(0)
    base = i * TR

    def row_copy(slot, k, r):
        return pltpu.make_async_copy(ys_ref.at[pl.ds(slot, 1)], buf_ref.at[k, pl.ds(r, 1)], sem)

    def issue(r, carry):
        row_copy(slot_ref[2 * (base + r)], 0, r).start()
        row_copy(slot_ref[2 * (base + r) + 1], 1, r).start()
        return carry

    lax.fori_loop(0, TR, issue, 0)

    def drain(r, carry):
        row_copy(0, 0, 0).wait()
        row_copy(0, 1, 0).wait()
        return carry

    lax.fori_loop(0, TR, drain, 0)

    gate = gate_ref[...]
    f = gate[:, 0:1] * buf_ref[0] + gate[:, 1:2] * buf_ref[1]

    @pl.when(i < n_prompt_tiles)
    def _():
        yp_ref[...] = _rms(xp_ref[...] + f, gf_ref[...])

    @pl.when(i >= n_prompt_tiles)
    def _():
        ysm_ref[...] = _rms(xs_ref[...] + f, gf_ref[...])


def _combine(slot_flat, xp, xs, gate, g_final, ys):
    n_p, d = xp.shape
    n_s = xs.shape[0]
    npt = n_p // TR
    kern = functools.partial(_combine_kernel, n_prompt_tiles=npt)
    pidx = lambda i, s: (jnp.minimum(i, npt - 1), 0)
    return pl.pallas_call(
        kern,
        out_shape=(jax.ShapeDtypeStruct((n_p, d), F32), jax.ShapeDtypeStruct((n_s, d), F32)),
        grid_spec=pltpu.PrefetchScalarGridSpec(
            num_scalar_prefetch=1,
            grid=((n_p + n_s) // TR,),
            in_specs=[
                pl.BlockSpec((TR, d), pidx),
                pl.BlockSpec((TR, d), lambda i, s: (0, 0)),
                pl.BlockSpec((TR, 2), lambda i, s: (i, 0)),
                pl.BlockSpec((1, d), lambda i, s: (0, 0)),
                pl.BlockSpec(memory_space=pl.ANY),
            ],
            out_specs=(
                pl.BlockSpec((TR, d), pidx),
                pl.BlockSpec((TR, d), lambda i, s: (0, 0)),
            ),
            scratch_shapes=[pltpu.VMEM((2, TR, d), F32), pltpu.SemaphoreType.DMA],
        ),
        compiler_params=_cparams(("arbitrary",)),
        name="moe_combine",
    )(slot_flat, xp, xs, gate, g_final.reshape(1, d), ys)


def _moe_layer_and_final_norm(xp, xs, g_ffn, w_router, wg, wu, wd, g_final):
    n = xp.shape[0] + xs.shape[0]
    n_e = w_router.shape[1]
    hn, eid, gate, rank, cnt = _router(xp, xs, g_ffn, w_router)
    tiles_e = (cnt[0] + TM_FFN - 1) // TM_FFN
    ends = jnp.cumsum(tiles_e)
    offs = (ends - tiles_e) * TM_FFN
    onehot = eid[..., None] == jnp.arange(n_e, dtype=I32)
    slot = (jnp.sum(jnp.where(onehot, offs, 0), axis=-1) + rank).reshape(-1).astype(I32)
    nt = (2 * n) // TM_FFN + n_e
    tile_expert = jnp.minimum(
        jnp.sum(jnp.arange(nt, dtype=I32)[:, None] >= ends[None, :], axis=-1), n_e - 1).astype(I32)
    n_valid = ends[-1:].astype(I32)
    xsorted = _scatter_rows(slot, hn, nt * TM_FFN)
    ys = _ffn_moe(tile_expert, n_valid, xsorted, wg, wu, wd)
    return _combine(slot, xp, xs, gate, g_final, ys)


def kernel(x_prompt, x_sample, mem_prompt, state_pool, cache_mem_k, cache_mem_v, g_mix_norm, w_in, g_v, w_spatial, b_spatial, w_pool, s_pool, g_mem, w_mem_kv, g_group, w_out, g_ffn_norm, w_dense_gate, w_dense_up, w_dense_down, w_router, w_moe_gate, w_moe_up, w_moe_down, g_final):
    depth = w_in.shape[0]
    assert depth == 2, "layer 0 dense, layer 1 experts followed by the final norm"
    b, t, d = x_prompt.shape
    n_req, n_new, _ = x_sample.shape
    d_a = g_v.shape[1]
    d_b = s_pool.shape[1]
    d_x = d - d_a - d_b
    n_mem = mem_prompt.shape[1]
    gd_a = d_a // A_HEADS

    w_in_b = w_in.astype(BF16)
    w_out_b = w_out.astype(BF16)
    mk_all, mv_all = _mem_kv(mem_prompt, g_mem, w_mem_kv.astype(BF16))

    xp = x_prompt
    xs = jnp.transpose(x_sample, (1, 0, 2)).reshape(n_new * n_req, d)
    state = jnp.transpose(state_pool, (0, 2, 1, 3))
    ck = cache_mem_k.reshape(depth, n_req, n_mem, d_x)
    cv = cache_mem_v.reshape(depth, n_req, n_mem, d_x)

    pool_p, pool_s, vrows_s = [], [], []
    for l in range(depth):
        wpool_bd = block_diag(*w_pool[l]).astype(BF16)
        bias_full = jnp.repeat(b_spatial[l].T, gd_a, axis=1)
        wv = jnp.repeat(jnp.transpose(w_spatial[l, :, :n_new, :n_new], (1, 2, 0)), gd_a,
                        axis=2).reshape(n_new * n_new, d_a)
        bv = jnp.repeat(b_spatial[l, :, :n_new].T, gd_a, axis=1)

        xp3, buf_p = _mix_prompt(xp.reshape(b, t, d), mk_all[l], mv_all[l], g_mix_norm[l], w_in_b[l],
                                 g_v[l], w_spatial[l], bias_full, wpool_bd, s_pool[l], g_group[l],
                                 w_out_b[l])
        xp = xp3.reshape(b * t, d)
        pool_p.append(buf_p)

        vn, a_out, b_out, qm, buf_s = _mix_sample_pre(xs, state[l], g_mix_norm[l], w_in_b[l], g_v[l],
                                                      wv, bv, wpool_bd, s_pool[l], n_new)
        o_all = _attn_sample(qm, ck[l], cv[l], n_new)
        xs = _mix_sample_post(xs, a_out, b_out, o_all, g_group[l], w_out_b[l], n_new)
        pool_s.append(buf_s)
        vrows_s.append(vn)

        if l == 0:
            wg, wu, wd = (w_dense_gate[0].astype(BF16), w_dense_up[0].astype(BF16),
                          w_dense_down[0].astype(BF16))
            xp = _ffn_dense(xp, g_ffn_norm[l], wg, wu, wd, TM_FFN)
            xs = _ffn_dense(xs, g_ffn_norm[l], wg, wu, wd, xs.shape[0])
        else:
            yp, ys = _moe_layer_and_final_norm(xp, xs, g_ffn_norm[l], w_router[0], w_moe_gate[0],
                                               w_moe_up[0], w_moe_down[0], g_final)

    y_prompt = yp.reshape(b, t, d)
    y_sample = jnp.transpose(ys.reshape(n_new, n_req, d), (1, 0, 2))
    pool_sample = jnp.transpose(jnp.stack(pool_s), (0, 2, 1, 3))
    gmlp_v = jnp.transpose(jnp.stack(vrows_s).reshape(depth, n_new, n_req, d_a), (0, 2, 1, 3))
    mem_shape = (depth, b, n_mem, X_HEADS, d_x // X_HEADS)
    return (y_prompt, y_sample, jnp.stack(pool_p), mk_all.reshape(mem_shape),
            mv_all.reshape(mem_shape), pool_sample, gmlp_v)
```

```python
import functools

import jax
import jax.numpy as jnp
from jax import lax
from jax.experimental import pallas as pl
from jax.experimental.pallas import tpu as pltpu
from jax.scipy.linalg import block_diag

F32 = jnp.float32
BF16 = jnp.bfloat16
I32 = jnp.int32

EPS = 1e-6
CHUNK = 128
A_HEADS = 4
POOL_WINDOWS = (2, 4, 8, 16)
POOL_BUF = max(POOL_WINDOWS) - 1
POOL_HDR = 16
X_HEADS = 4
N_EXPERTS = 8
PAST_LEN = 16384

TQ = 512
TM_FFN = 1024
TF_DENSE = 1408
TF_MOE = 512
TR = 512
ATT_G = 8
DMA_UNROLL = 8
VMEM_LIMIT = 56 * 1024 * 1024


def _cparams(sem):
    return pltpu.CompilerParams(dimension_semantics=sem, vmem_limit_bytes=VMEM_LIMIT)


def _rms(x, g):
    ms = jnp.mean(x * x, axis=-1, keepdims=True)
    return (x * lax.rsqrt(ms + EPS)) * g


def _dot(a, b):
    return jnp.dot(a, b, preferred_element_type=F32)


def _dot_nt(a, b):
    return lax.dot_general(a, b, (((1,), (1,)), ((), ())), preferred_element_type=F32)


def _lane_group_mask(width, n_groups, g):
    lane = lax.broadcasted_iota(I32, (1, width), 1)
    gw = width // n_groups
    return (lane >= g * gw) & (lane < (g + 1) * gw)


def _select_by_group(parts, width):
    out = parts[-1]
    for g in range(len(parts) - 2, -1, -1):
        out = jnp.where(_lane_group_mask(width, len(parts), g), parts[g], out)
    return out


def _softmax_rows(s):
    m = jnp.max(s, axis=-1, keepdims=True)
    e = jnp.exp(s - m)
    return e / jnp.sum(e, axis=-1, keepdims=True)


def _silu(x):
    return x * (1.0 / (1.0 + jnp.exp(-x)))


def _memkv_kernel(mem_ref, g_ref, w_ref, k_ref, v_ref):
    dx = k_ref.shape[-1]
    m = _rms(mem_ref[...], g_ref[...]).astype(BF16)
    kv = _dot(m, w_ref[...])
    k_ref[...] = kv[:, :dx]
    v_ref[...] = kv[:, dx:]


def _mem_kv(mem, g_mem, w_mem_kv_bf16):
    depth, d, two_dx = w_mem_kv_bf16.shape
    b, n_mem, _ = mem.shape
    dx = two_dx // 2
    out = jax.ShapeDtypeStruct((depth, b, n_mem, dx), F32)
    return pl.pallas_call(
        _memkv_kernel,
        out_shape=(out, out),
        grid=(depth, b),
        in_specs=[
            pl.BlockSpec((None, n_mem, d), lambda l, i: (i, 0, 0)),
            pl.BlockSpec((None, 1, d), lambda l, i: (l, 0, 0)),
            pl.BlockSpec((None, d, two_dx), lambda l, i: (l, 0, 0)),
        ],
        out_specs=(
            pl.BlockSpec((None, None, n_mem, dx), lambda l, i: (l, i, 0, 0)),
            pl.BlockSpec((None, None, n_mem, dx), lambda l, i: (l, i, 0, 0)),
        ),
        compiler_params=_cparams(("arbitrary", "arbitrary")),
        name="mem_kv",
    )(mem, g_mem.reshape(depth, 1, d), w_mem_kv_bf16)


def _mix_prompt_kernel(x_ref, gn_ref, win_ref, gv_ref, ws_ref, bias_ref, wpool_ref, spool_ref,
                       gg_ref, mk_ref, mv_ref, wout_ref, xo_ref, pool_ref, ext_ref, *, d_a, d_b, d_x):
    t = pl.program_id(1)
    n_t = pl.num_programs(1)
    tq = x_ref.shape[0]

    x = x_ref[...]
    h = _rms(x, gn_ref[...]).astype(BF16)
    p = _dot(h, win_ref[...])
    u = p[:, :d_a]
    v = p[:, d_a:2 * d_a]
    xb = p[:, 2 * d_a:2 * d_a + d_b]
    q = p[:, 2 * d_a + d_b:]

    ug = jax.nn.gelu(u)
    vn = _rms(jax.nn.gelu(v), gv_ref[...])
    row = lax.broadcasted_iota(I32, (CHUNK, A_HEADS * CHUNK), 0)
    col = lax.broadcasted_iota(I32, (CHUNK, A_HEADS * CHUNK), 1)
    wcat = jnp.concatenate([ws_ref[hh] for hh in range(A_HEADS)], axis=1)
    wcat = jnp.where((col & (CHUNK - 1)) <= row, wcat, 0.0).astype(BF16)
    head_masks = [_lane_group_mask(d_a, A_HEADS, hh) for hh in range(A_HEADS)]
    a_parts = []
    for c in range(tq // CHUNK):
        vc = vn[c * CHUNK:(c + 1) * CHUNK]
        vbd = jnp.concatenate([jnp.where(hm, vc, 0.0) for hm in head_masks], axis=0).astype(BF16)
        mixed = _dot(wcat, vbd) + bias_ref[...]
        a_parts.append(ug[c * CHUNK:(c + 1) * CHUNK] * mixed)
    a_out = jnp.concatenate(a_parts, axis=0)

    @pl.when(t == 0)
    def _():
        ext_ref[0:POOL_HDR, :] = jnp.zeros((POOL_HDR, d_b), F32)

    ext_ref[POOL_HDR:POOL_HDR + tq, :] = xb
    pos = t * tq + lax.broadcasted_iota(I32, (tq, 1), 0)
    acc = xb
    means = []
    k = 1
    for w in POOL_WINDOWS:
        while k < w:
            acc = acc + ext_ref[POOL_HDR - k:POOL_HDR - k + tq, :]
            k += 1
        cnt = jnp.minimum(pos + 1, w).astype(F32)
        means.append(acc / cnt)
    dlt = _select_by_group(means, d_b) - xb
    b_out = _dot(dlt.astype(BF16), wpool_ref[...]) * spool_ref[...]
    tail = ext_ref[tq:tq + POOL_HDR, :]
    ext_ref[0:POOL_HDR, :] = tail

    @pl.when(t == n_t - 1)
    def _():
        pool_ref[...] = tail

    kf = mk_ref[...]
    vf = mv_ref[...]
    qb = q.astype(BF16)
    scale = (d_x // X_HEADS) ** -0.5
    o = jnp.zeros((tq, d_x), F32)
    for hh in range(X_HEADS):
        hm = _lane_group_mask(d_x, X_HEADS, hh)
        kh = jnp.where(hm, kf, 0.0).astype(BF16)
        s = _dot_nt(qb, kh) * scale
        pr = _softmax_rows(s).astype(BF16)
        vh = jnp.where(hm, vf, 0.0).astype(BF16)
        o = o + _dot(pr, vh)

    gg = gg_ref[...]
    merged = jnp.concatenate([
        _rms(a_out, gg[:, :d_a]),
        _rms(b_out, gg[:, d_a:d_a + d_b]),
        _rms(o, gg[:, d_a + d_b:]),
    ], axis=-1).astype(BF16)
    xo_ref[...] = x + _dot(merged, wout_ref[...])


def _mix_prompt(x, mk, mv, g_norm, w_in_bf16, g_v, w_s, bias_full, wpool_bd, s_pool, g_group,
                w_out_bf16):
    b, t, d = x.shape
    d_in = w_in_bf16.shape[1]
    d_a = g_v.shape[0]
    d_b = s_pool.shape[0]
    d_x = d - d_a - d_b
    n_mem = mk.shape[1]
    full = lambda shape: pl.BlockSpec(shape, lambda i, j: (0,) * len(shape))
    kern = functools.partial(_mix_prompt_kernel, d_a=d_a, d_b=d_b, d_x=d_x)
    xo, pool = pl.pallas_call(
        kern,
        out_shape=(jax.ShapeDtypeStruct((b, t, d), F32),
                   jax.ShapeDtypeStruct((b, POOL_HDR, d_b), F32)),
        grid=(b, t // TQ),
        in_specs=[
            pl.BlockSpec((None, TQ, d), lambda i, j: (i, j, 0)),
            full((1, d)),
            full((d, d_in)),
            full((1, d_a)),
            full((A_HEADS, CHUNK, CHUNK)),
            full((CHUNK, d_a)),
            full((d_b, d_b)),
            full((1, d_b)),
            full((1, d)),
            pl.BlockSpec((None, n_mem, d_x), lambda i, j: (i, 0, 0)),
            pl.BlockSpec((None, n_mem, d_x), lambda i, j: (i, 0, 0)),
            full((d, d)),
        ],
        out_specs=(
            pl.BlockSpec((None, TQ, d), lambda i, j: (i, j, 0)),
            pl.BlockSpec((None, POOL_HDR, d_b), lambda i, j: (i, 0, 0)),
        ),
        scratch_shapes=[pltpu.VMEM((POOL_HDR + TQ, d_b), F32)],
        compiler_params=_cparams(("arbitrary", "arbitrary")),
        name="mix_prompt",
    )(x, g_norm.reshape(1, d), w_in_bf16, g_v.reshape(1, d_a), w_s, bias_full, wpool_bd,
      s_pool.reshape(1, d_b), g_group.reshape(1, d), mk, mv, w_out_bf16)
    return xo, pool[:, POOL_HDR - POOL_BUF:]


def _log2(n):
    assert n & (n - 1) == 0, n
    return n.bit_length() - 1


def _head_replication(n_req, n_new, transposed):
    hr = n_req * X_HEADS * n_new
    tr = n_new * n_req
    shape = (tr, hr) if transposed else (hr, tr)
    r = lax.broadcasted_iota(I32, shape, 1 if transposed else 0)
    c = lax.broadcasted_iota(I32, shape, 0 if transposed else 1)
    return c == (r & (n_new - 1)) * n_req + (r >> _log2(X_HEADS * n_new))


def _row_head_mask(rows, n_new, d_x):
    r = lax.broadcasted_iota(I32, (rows, d_x), 0)
    lane = lax.broadcasted_iota(I32, (rows, d_x), 1)
    return ((r >> _log2(n_new)) & (X_HEADS - 1)) == (lane >> _log2(d_x // X_HEADS))


def _mix_sample_pre_kernel(x_ref, gn_ref, win_ref, gv_ref, wv_ref, bv_ref, state_ref, wpool_ref,
                           spool_ref, vn_ref, a_ref, b_ref, qm_ref, pool_ref, *, n_new, d_a, d_b):
    n_req = x_ref.shape[0] // n_new
    h = _rms(x_ref[...], gn_ref[...]).astype(BF16)
    p = _dot(h, win_ref[...])
    ug = jax.nn.gelu(p[:, :d_a])
    vn = _rms(jax.nn.gelu(p[:, d_a:2 * d_a]), gv_ref[...])
    xb = p[:, 2 * d_a:2 * d_a + d_b]
    q = p[:, 2 * d_a + d_b:]
    d_x = q.shape[1]
    vn_ref[...] = vn
    slab = lambda arr, i: arr[i * n_req:(i + 1) * n_req]

    a_parts = []
    for t in range(n_new):
        mixed = bv_ref[t:t + 1, :]
        for s in range(t + 1):
            mixed = mixed + wv_ref[t * n_new + s:t * n_new + s + 1, :] * slab(vn, s)
        a_parts.append(slab(ug, t) * mixed)
    a_ref[...] = jnp.concatenate(a_parts, axis=0)

    suffix = [jnp.zeros((n_req, d_b), F32)]
    for j in range(1, POOL_BUF + 1):
        suffix.append(suffix[-1] + state_ref[POOL_BUF - j])
    d_parts = []
    for t in range(n_new):
        means = []
        for w in POOL_WINDOWS:
            acc = suffix[max(0, w - 1 - t)]
            for s in range(max(0, t - w + 1), t + 1):
                acc = acc + slab(xb, s)
            means.append(acc / float(min(PAST_LEN + t + 1, w)))
        d_parts.append(_select_by_group(means, d_b) - slab(xb, t))
    dlt = jnp.concatenate(d_parts, axis=0).astype(BF16)
    b_ref[...] = _dot(dlt, wpool_ref[...]) * spool_ref[...]

    for i in range(POOL_BUF - n_new):
        pool_ref[i] = state_ref[i + n_new]
    for t in range(n_new):
        pool_ref[POOL_BUF - n_new + t] = slab(xb, t)

    rep = jnp.where(_head_replication(n_req, n_new, False), 1.0, 0.0).astype(BF16)
    qm = _dot(rep, q.astype(BF16))
    qm_ref[...] = jnp.where(_row_head_mask(qm.shape[0], n_new, d_x), qm, 0.0).astype(BF16)


def _mix_sample_pre(x, state, g_norm, w_in_bf16, g_v, wv, bv, wpool_bd, s_pool, n_new):
    n, d = x.shape
    n_req = n // n_new
    d_a = g_v.shape[0]
    d_b = s_pool.shape[0]
    d_x = d - d_a - d_b
    kern = functools.partial(_mix_sample_pre_kernel, n_new=n_new, d_a=d_a, d_b=d_b)
    return pl.pallas_call(
        kern,
        out_shape=(
            jax.ShapeDtypeStruct((n, d_a), F32),
            jax.ShapeDtypeStruct((n, d_a), F32),
            jax.ShapeDtypeStruct((n, d_b), F32),
            jax.ShapeDtypeStruct((n_req * X_HEADS * n_new, d_x), BF16),
            jax.ShapeDtypeStruct((POOL_BUF, n_req, d_b), F32),
        ),
        compiler_params=pltpu.CompilerParams(vmem_limit_bytes=VMEM_LIMIT),
        name="mix_sample_pre",
    )(x, g_norm.reshape(1, d), w_in_bf16, g_v.reshape(1, d_a), wv, bv, state, wpool_bd,
      s_pool.reshape(1, d_b))


def _attn_sample_kernel(qm_ref, k_ref, v_ref, o_ref, *, rows_per_req, scale):
    for g in range(k_ref.shape[0]):
        qg = qm_ref[g * rows_per_req:(g + 1) * rows_per_req, :]
        s = _dot_nt(qg, k_ref[g].astype(BF16)) * scale
        pr = _softmax_rows(s).astype(BF16)
        o_ref[g * rows_per_req:(g + 1) * rows_per_req, :] = _dot(pr, v_ref[g].astype(BF16))


def _attn_sample(qm, k, v, n_new):
    n_req, n_mem, d_x = k.shape
    rpr = X_HEADS * n_new
    kern = functools.partial(_attn_sample_kernel, rows_per_req=rpr,
                             scale=(d_x // X_HEADS) ** -0.5)
    return pl.pallas_call(
        kern,
        out_shape=jax.ShapeDtypeStruct((n_req * rpr, d_x), F32),
        grid=(n_req // ATT_G,),
        in_specs=[
            pl.BlockSpec((ATT_G * rpr, d_x), lambda i: (i, 0)),
            pl.BlockSpec((ATT_G, n_mem, d_x), lambda i: (i, 0, 0)),
            pl.BlockSpec((ATT_G, n_mem, d_x), lambda i: (i, 0, 0)),
        ],
        out_specs=pl.BlockSpec((ATT_G * rpr, d_x), lambda i: (i, 0)),
        compiler_params=_cparams(("arbitrary",)),
        name="attn_sample",
    )(qm, k, v)


def _split_bf16(x):
    hi = x.astype(BF16)
    r1 = x - hi.astype(F32)
    mid = r1.astype(BF16)
    lo = (r1 - mid.astype(F32)).astype(BF16)
    return hi, mid, lo


def _mix_sample_post_kernel(x_ref, a_ref, b_ref, o_ref, gg_ref, wout_ref, xo_ref, *, n_new):
    n = x_ref.shape[0]
    d_a = a_ref.shape[1]
    d_b = b_ref.shape[1]
    d_x = o_ref.shape[1]
    om = jnp.where(_row_head_mask(o_ref.shape[0], n_new, d_x), o_ref[...], 0.0)
    rep_t = jnp.where(_head_replication(n // n_new, n_new, True), 1.0, 0.0).astype(BF16)
    hi, mid, lo = _split_bf16(om)
    x_att = _dot(rep_t, hi) + _dot(rep_t, mid) + _dot(rep_t, lo)
    gg = gg_ref[...]
    merged = jnp.concatenate([
        _rms(a_ref[...], gg[:, :d_a]),
        _rms(b_ref[...], gg[:, d_a:d_a + d_b]),
        _rms(x_att, gg[:, d_a + d_b:]),
    ], axis=-1).astype(BF16)
    xo_ref[...] = x_ref[...] + _dot(merged, wout_ref[...])


def _mix_sample_post(x, a_out, b_out, o_all, g_group, w_out_bf16, n_new):
    n, d = x.shape
    kern = functools.partial(_mix_sample_post_kernel, n_new=n_new)
    return pl.pallas_call(
        kern,
        out_shape=jax.ShapeDtypeStruct((n, d), F32),
        compiler_params=pltpu.CompilerParams(vmem_limit_bytes=VMEM_LIMIT),
        name="mix_sample_post",
    )(x, a_out, b_out, o_all, g_group.reshape(1, d), w_out_bf16)


def _ffn_dense_kernel(x_ref, g_ref, wg_ref, wu_ref, wd_ref, o_ref, hn_ref):
    j = pl.program_id(1)

    @pl.when(j == 0)
    def _():
        hn_ref[...] = _rms(x_ref[...], g_ref[...]).astype(BF16)

    hn = hn_ref[...]
    gate = _dot(hn, wg_ref[...].astype(BF16))
    up = _dot(hn, wu_ref[...].astype(BF16))
    y = _dot((_silu(gate) * up).astype(BF16), wd_ref[...].astype(BF16))

    @pl.when(j == 0)
    def _():
        o_ref[...] = x_ref[...] + y

    @pl.when(j > 0)
    def _():
        o_ref[...] += y


def _ffn_dense(x, g, wg, wu, wd, tm):
    n, d = x.shape
    f = wg.shape[1]
    return pl.pallas_call(
        _ffn_dense_kernel,
        out_shape=jax.ShapeDtypeStruct((n, d), F32),
        grid=(n // tm, f // TF_DENSE),
        in_specs=[
            pl.BlockSpec((tm, d), lambda i, j: (i, 0)),
            pl.BlockSpec((1, d), lambda i, j: (0, 0)),
            pl.BlockSpec((d, TF_DENSE), lambda i, j: (0, j)),
            pl.BlockSpec((d, TF_DENSE), lambda i, j: (0, j)),
            pl.BlockSpec((TF_DENSE, d), lambda i, j: (j, 0)),
        ],
        out_specs=pl.BlockSpec((tm, d), lambda i, j: (i, 0)),
        scratch_shapes=[pltpu.VMEM((tm, d), BF16)],
        compiler_params=_cparams(("arbitrary", "arbitrary")),
        name="ffn_dense",
    )(x, g.reshape(1, d), wg, wu, wd)


def _router_kernel(xp_ref, xs_ref, g_ref, wr_ref, hn_ref, eid_ref, gate_ref, rank_ref, cnt_ref,
                   carry_ref, *, n_prompt_tiles):
    i = pl.program_id(0)

    @pl.when(i == 0)
    def _():
        carry_ref[...] = jnp.zeros_like(carry_ref)

    x = jnp.where(i < n_prompt_tiles, xp_ref[...], xs_ref[...])
    hn = _rms(x, g_ref[...])
    hn_ref[...] = hn
    logits = jnp.dot(hn, wr_ref[...], preferred_element_type=F32,
                     precision=lax.Precision.HIGHEST)
    n_e = logits.shape[1]
    lane = lax.broadcasted_iota(I32, logits.shape, 1).astype(F32)
    m1 = jnp.max(logits, axis=-1, keepdims=True)
    i1 = jnp.min(jnp.where(logits == m1, lane, float(n_e)), axis=-1, keepdims=True)
    rest = jnp.where(lane == i1, -jnp.inf, logits)
    m2 = jnp.max(rest, axis=-1, keepdims=True)
    i2 = jnp.min(jnp.where(rest == m2, lane, float(n_e)), axis=-1, keepdims=True)
    e2 = jnp.exp(m2 - m1)
    den = 1.0 + e2
    two = lax.broadcasted_iota(I32, (logits.shape[0], 2), 1)
    eid_ref[...] = jnp.where(two == 0, i1, i2).astype(I32)
    gate_ref[...] = jnp.where(two == 0, 1.0 / den, e2 / den)

    onehot = jnp.where((lane == i1) | (lane == i2), 1.0, 0.0)
    tr = logits.shape[0]
    r = lax.broadcasted_iota(I32, (tr, tr), 0)
    c = lax.broadcasted_iota(I32, (tr, tr), 1)
    earlier = jnp.where(c < r, 1.0, 0.0).astype(BF16)
    before = _dot(earlier, onehot.astype(BF16)) + carry_ref[...]
    r1 = jnp.sum(jnp.where(lane == i1, before, 0.0), axis=-1, keepdims=True)
    r2 = jnp.sum(jnp.where(lane == i2, before, 0.0), axis=-1, keepdims=True)
    rank_ref[...] = jnp.where(two == 0, r1, r2).astype(I32)
    carry_ref[...] += jnp.sum(onehot, axis=0, keepdims=True)
    cnt_ref[...] = carry_ref[...].astype(I32)


def _router(xp, xs, g, w_router):
    n_p, d = xp.shape
    n_s = xs.shape[0]
    assert n_s == TR and n_p % TR == 0
    n = n_p + n_s
    npt = n_p // TR
    n_e = w_router.shape[1]
    kern = functools.partial(_router_kernel, n_prompt_tiles=npt)
    tok2 = lambda dt: jax.ShapeDtypeStruct((n, 2), dt)
    return pl.pallas_call(
        kern,
        out_shape=(jax.ShapeDtypeStruct((n, d), F32), tok2(I32), tok2(F32), tok2(I32),
                   jax.ShapeDtypeStruct((1, n_e), I32)),
        grid=(n // TR,),
        in_specs=[
            pl.BlockSpec((TR, d), lambda i: (jnp.minimum(i, npt - 1), 0)),
            pl.BlockSpec((TR, d), lambda i: (0, 0)),
            pl.BlockSpec((1, d), lambda i: (0, 0)),
            pl.BlockSpec((d, n_e), lambda i: (0, 0)),
        ],
        out_specs=(
            pl.BlockSpec((TR, d), lambda i: (i, 0)),
            pl.BlockSpec((TR, 2), lambda i: (i, 0)),
            pl.BlockSpec((TR, 2), lambda i: (i, 0)),
            pl.BlockSpec((TR, 2), lambda i: (i, 0)),
            pl.BlockSpec((1, n_e), lambda i: (0, 0)),
        ),
        scratch_shapes=[pltpu.VMEM((1, n_e), F32)],
        compiler_params=_cparams(("arbitrary",)),
        name="moe_router",
    )(xp, xs, g.reshape(1, d), w_router)


def _scatter_kernel(slot_ref, hn_ref, xs_in_ref, xs_ref, sem):
    del xs_in_ref
    base = pl.program_id(0) * TR

    def issue(r, carry):
        for k in range(2):
            slot = slot_ref[2 * (base + r) + k]
            pltpu.make_async_copy(hn_ref.at[pl.ds(r, 1)], xs_ref.at[pl.ds(slot, 1)], sem).start()
        return carry

    lax.fori_loop(0, TR, issue, 0, unroll=DMA_UNROLL)
    for k in range(2):
        pltpu.make_async_copy(hn_ref, xs_ref.at[pl.ds(0, TR)], sem).wait()


def _scatter_rows(slot_flat, hn, n_slots):
    n, d = hn.shape
    xs0 = jnp.zeros((n_slots, d), F32)
    return pl.pallas_call(
        _scatter_kernel,
        out_shape=jax.ShapeDtypeStruct((n_slots, d), F32),
        grid_spec=pltpu.PrefetchScalarGridSpec(
            num_scalar_prefetch=1,
            grid=(n // TR,),
            in_specs=[pl.BlockSpec((TR, d), lambda i, s: (i, 0)),
                      pl.BlockSpec(memory_space=pl.ANY)],
            out_specs=pl.BlockSpec(memory_space=pl.ANY),
            scratch_shapes=[pltpu.SemaphoreType.DMA],
        ),
        input_output_aliases={2: 0},
        compiler_params=_cparams(("arbitrary",)),
        name="moe_scatter",
    )(slot_flat, hn, xs0)


def _ffn_moe_kernel(te_ref, nv_ref, x_ref, wg_ref, wu_ref, wd_ref, o_ref, xb_ref):
    i = pl.program_id(0)
    j = pl.program_id(1)

    @pl.when(i < nv_ref[0])
    def _():
        @pl.when(j == 0)
        def _():
            xb_ref[...] = x_ref[...].astype(BF16)

        xb = xb_ref[...]
        gate = _dot(xb, wg_ref[...].astype(BF16))
        up = _dot(xb, wu_ref[...].astype(BF16))
        y = _dot((_silu(gate) * up).astype(BF16), wd_ref[...].astype(BF16))

        @pl.when(j == 0)
        def _():
            o_ref[...] = y

        @pl.when(j > 0)
        def _():
            o_ref[...] += y

    @pl.when((i >= nv_ref[0]) & (j == 0))
    def _():
        o_ref[...] = jnp.zeros_like(o_ref)


def _ffn_moe(tile_expert, n_valid, xs, wg, wu, wd, li):
    n_slots, d = xs.shape
    f = wg.shape[-1]
    nt = n_slots // TM_FFN
    nj = f // TF_MOE

    def row_idx(i, j, te, nv):
        return (jnp.minimum(i, nv[0] - 1), 0)

    def out_idx(i, j, te, nv):
        return (i, 0)

    def w_idx(i, j, te, nv):
        ok = i < nv[0]
        return (li, te[jnp.minimum(i, nv[0] - 1)], 0, jnp.where(ok, j, nj - 1))

    def wd_idx(i, j, te, nv):
        ok = i < nv[0]
        return (li, te[jnp.minimum(i, nv[0] - 1)], jnp.where(ok, j, nj - 1), 0)

    return pl.pallas_call(
        _ffn_moe_kernel,
        out_shape=jax.ShapeDtypeStruct((n_slots, d), F32),
        grid_spec=pltpu.PrefetchScalarGridSpec(
            num_scalar_prefetch=2,
            grid=(nt, nj),
            in_specs=[
                pl.BlockSpec((TM_FFN, d), row_idx),
                pl.BlockSpec((None, None, d, TF_MOE), w_idx),
                pl.BlockSpec((None, None, d, TF_MOE), w_idx),
                pl.BlockSpec((None, None, TF_MOE, d), wd_idx),
            ],
            out_specs=pl.BlockSpec((TM_FFN, d), out_idx),
            scratch_shapes=[pltpu.VMEM((TM_FFN, d), BF16)],
        ),
        compiler_params=_cparams(("arbitrary", "arbitrary")),
        name="ffn_moe",
    )(tile_expert, n_valid, xs, wg, wu, wd)


def _combine_kernel(slot_ref, xp_ref, xs_ref, gate_ref, gf_ref, ys_ref, yp_ref, ysm_ref, buf_ref,
                    sem, *, n_prompt_tiles):
    i = pl.program_id(0)
    base = i * TR

    def issue(r, carry):
        for k in range(2):
            slot = slot_ref[2 * (base + r) + k]
            pltpu.make_async_copy(ys_ref.at[pl.ds(slot, 1)], buf_ref.at[k, pl.ds(r, 1)], sem).start()
        return carry

    lax.fori_loop(0, TR, issue, 0, unroll=DMA_UNROLL)
    pltpu.make_async_copy(buf_ref, buf_ref, sem).wait()

    gate = gate_ref[...]
    f = gate[:, 0:1] * buf_ref[0] + gate[:, 1:2] * buf_ref[1]

    @pl.when(i < n_prompt_tiles)
    def _():
        yp_ref[...] = _rms(xp_ref[...] + f, gf_ref[...])

    @pl.when(i >= n_prompt_tiles)
    def _():
        ysm_ref[...] = _rms(xs_ref[...] + f, gf_ref[...])


def _combine(slot_flat, xp, xs, gate, g_final, ys):
    n_p, d = xp.shape
    n_s = xs.shape[0]
    npt = n_p // TR
    kern = functools.partial(_combine_kernel, n_prompt_tiles=npt)
    pidx = lambda i, s: (jnp.minimum(i, npt - 1), 0)
    return pl.pallas_call(
        kern,
        out_shape=(jax.ShapeDtypeStruct((n_p, d), F32), jax.ShapeDtypeStruct((n_s, d), F32)),
        grid_spec=pltpu.PrefetchScalarGridSpec(
            num_scalar_prefetch=1,
            grid=((n_p + n_s) // TR,),
            in_specs=[
                pl.BlockSpec((TR, d), pidx),
                pl.BlockSpec((TR, d), lambda i, s: (0, 0)),
                pl.BlockSpec((TR, 2), lambda i, s: (i, 0)),
                pl.BlockSpec((1, d), lambda i, s: (0, 0)),
                pl.BlockSpec(memory_space=pl.ANY),
            ],
            out_specs=(
                pl.BlockSpec((TR, d), pidx),
                pl.BlockSpec((TR, d), lambda i, s: (0, 0)),
            ),
            scratch_shapes=[pltpu.VMEM((2, TR, d), F32), pltpu.SemaphoreType.DMA],
        ),
        compiler_params=_cparams(("arbitrary",)),
        name="moe_combine",
    )(slot_flat, xp, xs, gate, g_final.reshape(1, d), ys)


def _moe_layer_and_final_norm(xp, xs, g_ffn, w_router, wg, wu, wd, li, g_final):
    n = xp.shape[0] + xs.shape[0]
    n_e = w_router.shape[1]
    hn, eid, gate, rank, cnt = _router(xp, xs, g_ffn, w_router)
    tiles_e = (cnt[0] + TM_FFN - 1) // TM_FFN
    ends = jnp.cumsum(tiles_e)
    offs = (ends - tiles_e) * TM_FFN
    onehot = eid[..., None] == jnp.arange(n_e, dtype=I32)
    slot = (jnp.sum(jnp.where(onehot, offs, 0), axis=-1) + rank).reshape(-1).astype(I32)
    nt = (2 * n) // TM_FFN + n_e
    tile_expert = jnp.minimum(
        jnp.sum(jnp.arange(nt, dtype=I32)[:, None] >= ends[None, :], axis=-1), n_e - 1).astype(I32)
    n_valid = ends[-1:].astype(I32)
    xsorted = _scatter_rows(slot, hn, nt * TM_FFN)
    ys = _ffn_moe(tile_expert, n_valid, xsorted, wg, wu, wd, li)
    return _combine(slot, xp, xs, gate, g_final, ys)


def kernel(x_prompt, x_sample, mem_prompt, state_pool, cache_mem_k, cache_mem_v, g_mix_norm, w_in, g_v, w_spatial, b_spatial, w_pool, s_pool, g_mem, w_mem_kv, g_group, w_out, g_ffn_norm, w_dense_gate, w_dense_up, w_dense_down, w_router, w_moe_gate, w_moe_up, w_moe_down, g_final):
    depth = w_in.shape[0]
    assert depth == 2, "layer 0 dense, layer 1 experts followed by the final norm"
    b, t, d = x_prompt.shape
    n_req, n_new, _ = x_sample.shape
    d_a = g_v.shape[1]
    d_b = s_pool.shape[1]
    d_x = d - d_a - d_b
    n_mem = mem_prompt.shape[1]
    gd_a = d_a // A_HEADS

    w_in_b = w_in.astype(BF16)
    w_out_b = w_out.astype(BF16)
    mk_all, mv_all = _mem_kv(mem_prompt, g_mem, w_mem_kv.astype(BF16))

    xp = x_prompt
    xs = jnp.transpose(x_sample, (1, 0, 2)).reshape(n_new * n_req, d)
    state = jnp.transpose(state_pool, (0, 2, 1, 3))
    ck = cache_mem_k.reshape(depth, n_req, n_mem, d_x)
    cv = cache_mem_v.reshape(depth, n_req, n_mem, d_x)

    pool_p, pool_s, vrows_s = [], [], []
    for l in range(depth):
        wpool_bd = block_diag(*w_pool[l]).astype(BF16)
        bias_full = jnp.repeat(b_spatial[l].T, gd_a, axis=1)
        wv = jnp.repeat(jnp.transpose(w_spatial[l, :, :n_new, :n_new], (1, 2, 0)), gd_a,
                        axis=2).reshape(n_new * n_new, d_a)
        bv = jnp.repeat(b_spatial[l, :, :n_new].T, gd_a, axis=1)

        xp3, buf_p = _mix_prompt(xp.reshape(b, t, d), mk_all[l], mv_all[l], g_mix_norm[l], w_in_b[l],
                                 g_v[l], w_spatial[l], bias_full, wpool_bd, s_pool[l], g_group[l],
                                 w_out_b[l])
        xp = xp3.reshape(b * t, d)
        pool_p.append(buf_p)

        vn, a_out, b_out, qm, buf_s = _mix_sample_pre(xs, state[l], g_mix_norm[l], w_in_b[l], g_v[l],
                                                      wv, bv, wpool_bd, s_pool[l], n_new)
        o_all = _attn_sample(qm, ck[l], cv[l], n_new)
        xs = _mix_sample_post(xs, a_out, b_out, o_all, g_group[l], w_out_b[l], n_new)
        pool_s.append(buf_s)
        vrows_s.append(vn)

        if l == 0:
            wg, wu, wd = (w_dense_gate[0].astype(BF16), w_dense_up[0].astype(BF16),
                          w_dense_down[0].astype(BF16))
            xp = _ffn_dense(xp, g_ffn_norm[l], wg, wu, wd, TM_FFN)
            xs = _ffn_dense(xs, g_ffn_norm[l], wg, wu, wd, xs.shape[0])
        else:
            yp, ys = _moe_layer_and_final_norm(xp, xs, g_ffn_norm[l], w_router[l // 2], w_moe_gate,
                                               w_moe_up, w_moe_down, l // 2, g_final)

    y_prompt = yp.reshape(b, t, d)
    y_sample = jnp.transpose(ys.reshape(n_new, n_req, d), (1, 0, 2))
    pool_sample = jnp.transpose(jnp.stack(pool_s), (0, 2, 1, 3))
    gmlp_v = jnp.transpose(jnp.stack(vrows_s).reshape(depth, n_new, n_req, d_a), (0, 2, 1, 3))
    mem_shape = (depth, b, n_mem, X_HEADS, d_x // X_HEADS)
    return (y_prompt, y_sample, jnp.stack(pool_p), mk_all.reshape(mem_shape),
            mv_all.reshape(mem_shape), pool_sample, gmlp_v)
```
